```python
import math
import jax, jax.numpy as jnp
from jax import lax
import numpy as np

D_MODEL = 1024
BATCH = 4
SEQ = 4096
DEPTH = 2

PLE_DIM = 256
GRID_W = 64
EPS = 1e-6
ROPE_THETA = 10000.0

HA = 4
DH_A = 64
DV_A = 2 * DH_A
WIDTH_A = HA * DV_A
QBLK = 128

CB = 512
CONV_W = 31

HC = 8
DC = 64
WIDTH_C = HC * DC
NA_KR = 8
NA_KC = 16

D_FF = int(math.ceil(8 * D_MODEL / 3 / 256) * 256)

COLS = [
    HA * 2 * DH_A,
    HA * 2 * DH_A,
    WIDTH_A,
    2 * CB,
    WIDTH_C,
    WIDTH_C,
    WIDTH_C,
    3 * D_MODEL,
]
SPLITS = list(np.cumsum(COLS)[:-1].tolist())
D_IN = int(sum(COLS))

kernel_name = "hybrid_diffattn_conformer_natten_block"


def rmsnorm(x, g):
    x32 = x.astype(jnp.float32)
    y = x32 * lax.rsqrt(jnp.mean(x32 * x32, axis=-1, keepdims=True) + EPS)
    return (y * g.astype(jnp.float32)).astype(x.dtype)


def layernorm(x, g, b):
    x32 = x.astype(jnp.float32)
    mu = jnp.mean(x32, axis=-1, keepdims=True)
    var = jnp.mean(jnp.square(x32 - mu), axis=-1, keepdims=True)
    y = (x32 - mu) * lax.rsqrt(var + EPS)
    return (y * g.astype(jnp.float32) + b.astype(jnp.float32)).astype(x.dtype)


def rope_tables(seq, dim):
    inv = 1.0 / (ROPE_THETA ** (jnp.arange(0, dim, 2, dtype=jnp.float32) / dim))
    ang = jnp.arange(seq, dtype=jnp.float32)[:, None] * inv[None, :]
    ang = jnp.concatenate([ang, ang], axis=-1)
    return jnp.cos(ang), jnp.sin(ang)


def apply_rope(x, cos, sin):
    half = x.shape[-1] // 2
    x1, x2 = x[..., :half], x[..., half:]
    rot = jnp.concatenate([-x2, x1], axis=-1)
    return (x.astype(jnp.float32) * cos + rot.astype(jnp.float32) * sin).astype(x.dtype)


def diff_attention(qa, ka, va, lq1, lk1, lq2, lk2, subln_g, lam_init, cos, sin):
    B, S, _ = qa.shape
    q = qa.reshape(B, S, HA, 2, DH_A).transpose(0, 2, 3, 1, 4)
    k = ka.reshape(B, S, HA, 2, DH_A).transpose(0, 2, 3, 1, 4)
    v = va.reshape(B, S, HA, DV_A).transpose(0, 2, 1, 3)
    q = apply_rope(q, cos, sin)
    k = apply_rope(k, cos, sin)
    f32 = jnp.float32
    lam = (jnp.exp(jnp.sum(lq1.astype(f32) * lk1.astype(f32)))
           - jnp.exp(jnp.sum(lq2.astype(f32) * lk2.astype(f32))) + lam_init)
    scale = DH_A ** -0.5
    nb = S // QBLK
    qb = q.reshape(B, HA, 2, nb, QBLK, DH_A).transpose(3, 0, 1, 2, 4, 5)

    def block(qi):
        s = jnp.einsum('bhcqd,bhckd->bhcqk', qi, k).astype(f32) * scale
        a = jax.nn.softmax(s, axis=-1)
        a = (a[:, :, 0] - lam * a[:, :, 1]).astype(v.dtype)
        return jnp.einsum('bhqk,bhkd->bhqd', a, v)

    o = lax.map(block, qb)
    o = o.transpose(1, 2, 0, 3, 4).reshape(B, HA, S, DV_A)
    o = rmsnorm(o, subln_g) * (1.0 - lam_init)
    return o.transpose(0, 2, 1, 3).reshape(B, S, WIDTH_A)


def conformer_conv(u, conv_w, conv_b, cln_g, cln_b):
    a, g = jnp.split(u, 2, axis=-1)
    z = a * jax.nn.sigmoid(g)
    z = lax.conv_general_dilated(
        z, conv_w.astype(z.dtype), window_strides=(1,),
        padding=[(CONV_W // 2, CONV_W // 2)],
        dimension_numbers=('NWC', 'WIO', 'NWC'),
        feature_group_count=CB)
    z = z + conv_b
    z = layernorm(z, cln_g, cln_b)
    return jax.nn.silu(z)


def neighbourhood_attention(qc, kc, vc, rpb):
    B, S, _ = qc.shape
    R = S // GRID_W
    KR = min(NA_KR, R)
    q = qc.reshape(B, R, GRID_W, HC, DC).transpose(1, 0, 3, 2, 4)
    k = kc.reshape(B, R, GRID_W, HC, DC).transpose(0, 3, 1, 2, 4)
    v = vc.reshape(B, R, GRID_W, HC, DC).transpose(0, 3, 1, 2, 4)
    cols = jnp.arange(GRID_W)
    cs = jnp.clip(cols - NA_KC // 2, 0, GRID_W - NA_KC)
    col_idx = cs[:, None] + jnp.arange(NA_KC)[None, :]
    col_off = col_idx - cols[:, None] + (NA_KC - 1)
    scale = DC ** -0.5

    def row(args):
        q_r, r = args
        rstart = jnp.clip(r - KR // 2, 0, R - KR)
        kr = lax.dynamic_slice_in_dim(k, rstart, KR, axis=2)
        vr = lax.dynamic_slice_in_dim(v, rstart, KR, axis=2)
        kn = kr[:, :, :, col_idx]
        vn = vr[:, :, :, col_idx]
        s = jnp.einsum('bhwd,bhrwcd->bhwrc', q_r, kn).astype(jnp.float32) * scale
        row_off = rstart + jnp.arange(KR) - r + (NA_KR - 1)
        bias = rpb[:, row_off[:, None, None], col_off[None, :, :]]
        s = s + bias.transpose(0, 2, 1, 3)[None].astype(jnp.float32)
        a = jax.nn.softmax(s.reshape(B, HC, GRID_W, KR * NA_KC), axis=-1)
        a = a.reshape(B, HC, GRID_W, KR, NA_KC).astype(vn.dtype)
        return jnp.einsum('bhwrc,bhrwcd->bhwd', a, vn)

    o = lax.map(row, (q, jnp.arange(R)))
    return o.transpose(1, 0, 3, 2, 4).reshape(B, S, WIDTH_C)


def setup_inputs(seed: int = 0) -> dict:
    key = jax.random.key(seed)
    ks = iter(jax.random.split(key, 32))
    L, D = DEPTH, D_MODEL

    def nrm(shape, scale):
        return jax.random.normal(next(ks), shape, jnp.float32) * scale

    def gain(shape):
        return 1.0 + nrm(shape, 0.02)

    return {
        "x": nrm((BATCH, SEQ, D), 1.0),
        "p": nrm((DEPTH, BATCH, SEQ, PLE_DIM), 1.0),
        "norm_mix": gain((L, D)),
        "w_in": nrm((L, D, D_IN), D ** -0.5),
        "lam_q1": nrm((L, DH_A), 0.1),
        "lam_k1": nrm((L, DH_A), 0.1),
        "lam_q2": nrm((L, DH_A), 0.1),
        "lam_k2": nrm((L, DH_A), 0.1),
        "subln_g": gain((L, DV_A)),
        "conv_w": nrm((L, CONV_W, 1, CB), CONV_W ** -0.5),
        "conv_b": nrm((L, CB), 0.02),
        "cln_g": gain((L, CB)),
        "cln_b": nrm((L, CB), 0.02),
        "rpb": nrm((L, HC, 2 * NA_KR - 1, 2 * NA_KC - 1), 0.02),
        "w_br_a": nrm((L, WIDTH_A, D), WIDTH_A ** -0.5),
        "w_br_b": nrm((L, CB, D), CB ** -0.5),
        "w_br_c": nrm((L, WIDTH_C, D), WIDTH_C ** -0.5),
        "w_o": nrm((L, D, D), D ** -0.5),
        "norm_ffn": gain((L, D)),
        "w_ffn_gate": nrm((L, D, D_FF), D ** -0.5),
        "w_ffn_up": nrm((L, D, D_FF), D ** -0.5),
        "w_ffn_down": nrm((L, D_FF, D), D_FF ** -0.5),
        "norm_ple": gain((L, D)),
        "w_ple_in": nrm((L, PLE_DIM, D), PLE_DIM ** -0.5),
        "w_ple_gate": nrm((L, D, D), D ** -0.5),
        "norm_final": gain((D,)),
    }


def reference(x, p, norm_mix, w_in, lam_q1, lam_k1, lam_q2, lam_k2, subln_g,
              conv_w, conv_b, cln_g, cln_b, rpb, w_br_a, w_br_b, w_br_c, w_o,
              norm_ffn, w_ffn_gate, w_ffn_up, w_ffn_down, norm_ple, w_ple_in,
              w_ple_gate, norm_final):
    S = x.shape[1]
    cos, sin = rope_tables(S, DH_A)
    for i in range(DEPTH):
        lam_init = 0.8 - 0.6 * math.exp(-0.3 * i)
        h = rmsnorm(x, norm_mix[i])
        proj = h @ w_in[i]
        qa, ka, va, glu_in, qc, kc, vc, gates = jnp.split(proj, SPLITS, axis=-1)
        ya = diff_attention(qa, ka, va, lam_q1[i], lam_k1[i], lam_q2[i], lam_k2[i],
                            subln_g[i], lam_init, cos, sin)
        yb = conformer_conv(glu_in, conv_w[i], conv_b[i], cln_g[i], cln_b[i])
        yc = neighbourhood_attention(qc, kc, vc, rpb[i])
        ga, gb, gc = jnp.split(jax.nn.sigmoid(gates), 3, axis=-1)
        merged = (ga * (ya @ w_br_a[i]) + gb * (yb @ w_br_b[i])
                  + gc * (yc @ w_br_c[i]))
        x = x + merged @ w_o[i]
        h = rmsnorm(x, norm_ffn[i])
        x = x + (jax.nn.silu(h @ w_ffn_gate[i]) * (h @ w_ffn_up[i])) @ w_ffn_down[i]
        h = rmsnorm(x, norm_ple[i])
        x = x + jax.nn.sigmoid(h @ w_ple_gate[i]) * (p[i] @ w_ple_in[i])
    return rmsnorm(x, norm_final)
```

```python
import functools
import math

import jax
import jax.numpy as jnp
from jax import lax
from jax.experimental import pallas as pl
from jax.experimental.pallas import tpu as pltpu

F32 = jnp.float32
BF16 = jnp.bfloat16

D_MODEL = 1024
DEPTH = 2
PLE_DIM = 256
GRID_W = 64
EPS = 1e-6
ROPE_THETA = 10000.0
HA = 4
DH_A = 64
DV_A = 2 * DH_A
WIDTH_A = HA * DV_A
CB = 512
CONV_W = 31
HC = 8
DC = 64
WIDTH_C = HC * DC
NA_KR = 8
NA_KC = 16
D_FF = 2816
D_IN = 7168
OFF_QA, OFF_KA, OFF_VA, OFF_GLU, OFF_QC, OFF_KC, OFF_VC, OFF_GATES = 0, 512, 1024, 1536, 2560, 3072, 3584, 4096

LANES = 128
VMEM_LIMIT_BYTES = 56 * 1024 * 1024

LOG2E = math.log2(math.e)
NEG_BIG = -1e30

TM_PROJ = 512
TQ_A = 512
TK_A = 1024
TS_CONV = 256
RC_CONV = 32
CONV_HALO = 16
NA_QROWS = 4
NA_KROWS = 12
TM_POST = 256


def _dot(a, b):
    return jnp.dot(a, b, preferred_element_type=F32)


def _sigmoid(t):
    return 1.0 / (1.0 + jnp.exp(-t))


def _rms(t, g):
    return t * lax.rsqrt(jnp.mean(t * t, axis=-1, keepdims=True) + EPS) * g


def _const_spec(shape):
    nd = len(shape)
    return pl.BlockSpec(shape, lambda *_: (0,) * nd, pipeline_mode=pl.Buffered(1))


def _params(n_axes):
    return pltpu.CompilerParams(
        dimension_semantics=("arbitrary",) * n_axes, vmem_limit_bytes=VMEM_LIMIT_BYTES)


def _proj_kernel(x_ref, g_ref, w_ref, cos_ref, sin_ref,
                 qa_ref, kat_ref, va_ref, z_ref, qc_ref, kct_ref, vc_ref, gates_ref):
    h = _rms(x_ref[...], g_ref[...]).astype(BF16)

    def mm(c0, n):
        return _dot(h, w_ref[:, c0:c0 + n])

    cos = cos_ref[...]
    sin = sin_ref[...]
    lane = lax.broadcasted_iota(jnp.int32, cos.shape, 1)
    low_half = (lane & (DH_A // 2)) == 0

    def rope(t):
        rot = jnp.where(low_half, pltpu.roll(t, LANES - DH_A // 2, 1), pltpu.roll(t, DH_A // 2, 1))
        return t * cos + rot * sin

    q = mm(OFF_QA, 512)
    k = mm(OFF_KA, 512)
    for c in range(HA):
        sl = slice(c * LANES, (c + 1) * LANES)
        qa_ref[:, sl] = (rope(q[:, sl]) * (DH_A ** -0.5 * LOG2E)).astype(BF16)
        kat_ref[0, sl, :] = rope(k[:, sl]).T.astype(BF16)
    va_ref[...] = mm(OFF_VA, 512).astype(BF16)
    z_ref[...] = mm(OFF_GLU, CB) * _sigmoid(mm(OFF_GLU + CB, CB))
    qc_ref[...] = (mm(OFF_QC, 512) * (DC ** -0.5 * LOG2E)).astype(BF16)
    kc = mm(OFF_KC, 512)
    for c in range(WIDTH_C // LANES):
        sl = slice(c * LANES, (c + 1) * LANES)
        kct_ref[0, sl, :] = kc[:, sl].T.astype(BF16)
    vc_ref[...] = mm(OFF_VC, 512).astype(BF16)
    for c in range(3 * D_MODEL // 512):
        gates_ref[:, c * 512:(c + 1) * 512] = _sigmoid(mm(OFF_GATES + c * 512, 512)).astype(BF16)


def _proj(xf, g, w, cos, sin, batch, seq):
    t = xf.shape[0]
    tm = TM_PROJ
    ns = seq // tm
    row = lambda i: (i, 0)
    tposed = lambda i: (i // ns, 0, i % ns)
    out_shape = (
        jax.ShapeDtypeStruct((t, 512), BF16),
        jax.ShapeDtypeStruct((batch, 512, seq), BF16),
        jax.ShapeDtypeStruct((t, WIDTH_A), BF16),
        jax.ShapeDtypeStruct((t, CB), F32),
        jax.ShapeDtypeStruct((t, WIDTH_C), BF16),
        jax.ShapeDtypeStruct((batch, WIDTH_C, seq), BF16),
        jax.ShapeDtypeStruct((t, WIDTH_C), BF16),
        jax.ShapeDtypeStruct((t, 3 * D_MODEL), BF16),
    )
    out_specs = (
        pl.BlockSpec((tm, 512), row),
        pl.BlockSpec((1, 512, tm), tposed),
        pl.BlockSpec((tm, WIDTH_A), row),
        pl.BlockSpec((tm, CB), row),
        pl.BlockSpec((tm, WIDTH_C), row),
        pl.BlockSpec((1, WIDTH_C, tm), tposed),
        pl.BlockSpec((tm, WIDTH_C), row),
        pl.BlockSpec((tm, 3 * D_MODEL), row),
    )
    in_specs = [
        pl.BlockSpec((tm, D_MODEL), row),
        _const_spec((1, D_MODEL)),
        _const_spec((D_MODEL, D_IN)),
        pl.BlockSpec((tm, LANES), lambda i: (i % ns, 0)),
        pl.BlockSpec((tm, LANES), lambda i: (i % ns, 0)),
    ]
    return pl.pallas_call(
        _proj_kernel, grid=(t // tm,), in_specs=in_specs, out_specs=out_specs,
        out_shape=out_shape, compiler_params=_params(1), name="proj",
    )(xf, g, w, cos, sin)


def _diffattn_kernel(q_ref, kt_ref, v_ref, lq1_ref, lk1_ref, lq2_ref, lk2_ref, g_ref,
                     o_ref, vext_ref, m_ref, acc_ref, *, lam_init, seq):
    tq = q_ref.shape[0]

    @pl.when(pl.program_id(2) == 0)
    def _():
        vext_ref[:, :DV_A] = v_ref[...]
        vext_ref[:, DV_A:] = jnp.ones((seq, DV_A), BF16)

    q = q_ref[...].astype(F32)
    lane = lax.broadcasted_iota(jnp.int32, q.shape, 1)
    qs = jnp.concatenate(
        [jnp.where(lane < DH_A, q, 0.0), jnp.where(lane >= DH_A, q, 0.0)], axis=0).astype(BF16)

    m_ref[...] = jnp.full(m_ref.shape, NEG_BIG, F32)
    acc_ref[...] = jnp.zeros(acc_ref.shape, F32)

    def chunk(kc, carry):
        off = pl.multiple_of(kc * TK_A, TK_A)
        s = _dot(qs, kt_ref[0, :, pl.ds(off, TK_A)])
        m_prev = m_ref[...]
        m_new = jnp.maximum(m_prev, jnp.max(s, axis=1, keepdims=True))
        alpha = jnp.exp2(m_prev - m_new)
        p = jnp.exp2(s - jnp.concatenate([m_new] * (TK_A // LANES), axis=1))
        pv = _dot(p.astype(BF16), vext_ref[pl.ds(off, TK_A), :])
        acc_ref[...] = acc_ref[...] * jnp.concatenate([alpha, alpha], axis=1) + pv
        m_ref[...] = m_new
        return carry

    lax.fori_loop(0, seq // TK_A, chunk, 0)

    acc = acc_ref[...]
    o1 = acc[:tq, :DV_A] / acc[:tq, DV_A:]
    o2 = acc[tq:, :DV_A] / acc[tq:, DV_A:]
    lam = (jnp.exp(jnp.sum(lq1_ref[...] * lk1_ref[...], axis=-1, keepdims=True))
           - jnp.exp(jnp.sum(lq2_ref[...] * lk2_ref[...], axis=-1, keepdims=True)) + lam_init)
    o = o1 - lam * o2
    o_ref[...] = (_rms(o, g_ref[...]) * (1.0 - lam_init)).astype(BF16)


def _diffattn(qa, kat, va, lq1, lk1, lq2, lk2, g, lam_init, batch, seq):
    t = qa.shape[0]
    nq = seq // TQ_A
    kern = functools.partial(_diffattn_kernel, lam_init=lam_init, seq=seq)
    small = lambda n: pl.BlockSpec((1, n), lambda b, h, i: (0, 0))
    return pl.pallas_call(
        kern, grid=(batch, HA, nq),
        in_specs=[
            pl.BlockSpec((TQ_A, DV_A), lambda b, h, i: (b * nq + i, h)),
            pl.BlockSpec((1, DV_A, seq), lambda b, h, i: (b, h, 0)),
            pl.BlockSpec((seq, DV_A), lambda b, h, i: (b, h)),
            small(DH_A), small(DH_A), small(DH_A), small(DH_A), small(DV_A),
        ],
        out_specs=pl.BlockSpec((TQ_A, DV_A), lambda b, h, i: (b * nq + i, h)),
        out_shape=jax.ShapeDtypeStruct((t, WIDTH_A), BF16),
        scratch_shapes=[
            pltpu.VMEM((seq, 2 * DV_A), BF16),
            pltpu.VMEM((2 * TQ_A, LANES), F32),
            pltpu.VMEM((2 * TQ_A, 2 * DV_A), F32),
        ],
        compiler_params=_params(3), name="diffattn",
    )(qa, kat, va, lq1, lk1, lq2, lk2, g)


def _conv_kernel(zp_ref, z_ref, zn_ref, w_ref, b_ref, g_ref, beta_ref, o_ref, buf_ref):
    i = pl.program_id(1)
    ts = z_ref.shape[1]
    buf_ref[0:CONV_HALO, :] = jnp.where(i > 0, zp_ref[0], 0.0)
    buf_ref[CONV_HALO:CONV_HALO + ts, :] = z_ref[0]
    buf_ref[CONV_HALO + ts:, :] = jnp.where(i < pl.num_programs(1) - 1, zn_ref[0], 0.0)
    base = CONV_HALO - CONV_W // 2
    for r in range(ts // RC_CONV):
        acc = jnp.zeros((RC_CONV, CB), F32)
        for k in range(CONV_W):
            r0 = base + r * RC_CONV + k
            acc = acc + buf_ref[r0:r0 + RC_CONV, :] * w_ref[k:k + 1, :]
        y = acc + b_ref[...]
        mu = jnp.mean(y, axis=-1, keepdims=True)
        d = y - mu
        yn = d * lax.rsqrt(jnp.mean(d * d, axis=-1, keepdims=True) + EPS) * g_ref[...] + beta_ref[...]
        o_ref[r * RC_CONV:(r + 1) * RC_CONV, :] = (yn * _sigmoid(yn)).astype(BF16)


def _conformer(z, w, b, g, beta, batch, seq):
    ts = TS_CONV
    ns = seq // ts
    hb = ts // CONV_HALO
    nh = seq // CONV_HALO
    z3 = z.reshape(batch, seq, CB)
    vec = lambda: pl.BlockSpec((1, CB), lambda bb, i: (0, 0))
    return pl.pallas_call(
        _conv_kernel, grid=(batch, ns),
        in_specs=[
            pl.BlockSpec((1, CONV_HALO, CB), lambda bb, i: (bb, jnp.maximum(i * hb - 1, 0), 0)),
            pl.BlockSpec((1, ts, CB), lambda bb, i: (bb, i, 0)),
            pl.BlockSpec((1, CONV_HALO, CB), lambda bb, i: (bb, jnp.minimum((i + 1) * hb, nh - 1), 0)),
            pl.BlockSpec((CONV_W, CB), lambda bb, i: (0, 0)),
            vec(), vec(), vec(),
        ],
        out_specs=pl.BlockSpec((ts, CB), lambda bb, i: (bb * ns + i, 0)),
        out_shape=jax.ShapeDtypeStruct((batch * seq, CB), BF16),
        scratch_shapes=[pltpu.VMEM((ts + 2 * CONV_HALO, CB), F32)],
        compiler_params=_params(2), name="conformer",
    )(z3, z3, z3, w, b, g, beta)


def _natten_kernel(q_ref, kt_ref, v_ref, bias_ref, o_ref, *, rows):
    rb = pl.program_id(2)
    nq = NA_QROWS * GRID_W
    nk = NA_KROWS * GRID_W
    wstart = jnp.clip(rb * NA_QROWS - NA_KR // 2, 0, rows - NA_KROWS)
    off = pl.multiple_of(wstart * GRID_W, NA_QROWS * GRID_W)

    q = q_ref[...].astype(F32)
    lane = lax.broadcasted_iota(jnp.int32, q.shape, 1)
    qs = jnp.concatenate(
        [jnp.where(lane < DC, q, 0.0), jnp.where(lane >= DC, q, 0.0)], axis=0).astype(BF16)
    s = _dot(qs, kt_ref[0, :, pl.ds(off, nk)])
    s = s + bias_ref[...].reshape(2 * nq, nk)
    m = jnp.max(s, axis=-1, keepdims=True)
    p = jnp.exp2(s - m)
    l = jnp.sum(p, axis=-1, keepdims=True)
    o = _dot(p.astype(BF16), v_ref[pl.ds(off, nk), :]) / l
    o_ref[...] = jnp.where(lane < DC, o[:nq], o[nq:]).astype(BF16)


def _natten_bias(rpb, rows):
    qrow0 = jnp.array([0, NA_QROWS, rows - NA_QROWS])
    wstart = jnp.clip(qrow0 - NA_KR // 2, 0, rows - NA_KROWS)
    r = qrow0[:, None] + jnp.arange(NA_QROWS)[None, :]
    rstart = jnp.clip(r - NA_KR // 2, 0, rows - NA_KR)
    kr = wstart[:, None] + jnp.arange(NA_KROWS)[None, :]
    row_ok = (kr[:, None, :] >= rstart[:, :, None]) & (kr[:, None, :] < rstart[:, :, None] + NA_KR)
    row_off = jnp.clip(kr[:, None, :] - r[:, :, None] + (NA_KR - 1), 0, 2 * NA_KR - 2)
    c = jnp.arange(GRID_W)
    cs = jnp.clip(c - NA_KC // 2, 0, GRID_W - NA_KC)
    col_ok = (c[None, :] >= cs[:, None]) & (c[None, :] < cs[:, None] + NA_KC)
    col_off = jnp.clip(c[None, :] - c[:, None] + (NA_KC - 1), 0, 2 * NA_KC - 2)
    vals = rpb[:, row_off[:, :, None, :, None], col_off[None, None, :, None, :]]
    ok = row_ok[:, :, None, :, None] & col_ok[None, None, :, None, :]
    bias = jnp.where(ok[None], vals * LOG2E, NEG_BIG)
    return bias.reshape(HC, 3, NA_QROWS * GRID_W, NA_KROWS * GRID_W).astype(F32)


def _natten(qc, kct, vc, bias, batch, seq):
    t = qc.shape[0]
    rows = seq // GRID_W
    nrb = rows // NA_QROWS
    nq = NA_QROWS * GRID_W
    nk = NA_KROWS * GRID_W
    kern = functools.partial(_natten_kernel, rows=rows)

    def variant(rb):
        return jnp.where(rb == 0, 0, jnp.where(rb == nrb - 1, 2, 1))

    return pl.pallas_call(
        kern, grid=(HC // 2, batch, nrb),
        in_specs=[
            pl.BlockSpec((nq, LANES), lambda hp, b, rb: (b * nrb + rb, hp)),
            pl.BlockSpec((1, LANES, seq), lambda hp, b, rb: (b, hp, 0)),
            pl.BlockSpec((seq, LANES), lambda hp, b, rb: (b, hp)),
            pl.BlockSpec((2, 1, nq, nk), lambda hp, b, rb: (hp, variant(rb), 0, 0)),
        ],
        out_specs=pl.BlockSpec((nq, LANES), lambda hp, b, rb: (b * nrb + rb, hp)),
        out_shape=jax.ShapeDtypeStruct((t, WIDTH_C), BF16),
        compiler_params=_params(3), name="natten",
    )(qc, kct, vc, bias)


def _post_kernel(x_ref, ya_ref, yb_ref, yc_ref, gates_ref, p_ref,
                 wa_ref, wb_ref, wc_ref, wo_ref, nffn_ref, wg_ref, wu_ref, wd_ref,
                 nple_ref, wpg_ref, wpi_ref, nfin_ref, o_ref, *, final):
    x = x_ref[...]
    d = D_MODEL
    merged = (gates_ref[:, 0:d].astype(F32) * _dot(ya_ref[...], wa_ref[...])
              + gates_ref[:, d:2 * d].astype(F32) * _dot(yb_ref[...], wb_ref[...])
              + gates_ref[:, 2 * d:3 * d].astype(F32) * _dot(yc_ref[...], wc_ref[...]))
    x = x + _dot(merged.astype(BF16), wo_ref[...])

    h = _rms(x, nffn_ref[...]).astype(BF16)
    gate = _dot(h, wg_ref[...])
    up = _dot(h, wu_ref[...])
    act = (gate * _sigmoid(gate) * up).astype(BF16)
    x = x + _dot(act, wd_ref[...])

    h = _rms(x, nple_ref[...]).astype(BF16)
    sg = _sigmoid(_dot(h, wpg_ref[...]))
    x = x + sg * _dot(p_ref[...].astype(BF16), wpi_ref[...])
    if final:
        x = _rms(x, nfin_ref[...])
    o_ref[...] = x


def _post(xf, ya, yb, yc, gates, pf, wa, wb, wc, wo, nffn, wg, wu, wd, nple, wpg, wpi, nfin, final):
    t = xf.shape[0]
    tm = TM_POST
    row = lambda n: pl.BlockSpec((tm, n), lambda i: (i, 0))
    kern = functools.partial(_post_kernel, final=final)
    return pl.pallas_call(
        kern, grid=(t // tm,),
        in_specs=[
            row(D_MODEL), row(WIDTH_A), row(CB), row(WIDTH_C), row(3 * D_MODEL), row(PLE_DIM),
            _const_spec((WIDTH_A, D_MODEL)), _const_spec((CB, D_MODEL)), _const_spec((WIDTH_C, D_MODEL)),
            _const_spec((D_MODEL, D_MODEL)), _const_spec((1, D_MODEL)),
            _const_spec((D_MODEL, D_FF)), _const_spec((D_MODEL, D_FF)), _const_spec((D_FF, D_MODEL)),
            _const_spec((1, D_MODEL)), _const_spec((D_MODEL, D_MODEL)), _const_spec((PLE_DIM, D_MODEL)),
            _const_spec((1, D_MODEL)),
        ],
        out_specs=row(D_MODEL),
        out_shape=jax.ShapeDtypeStruct((t, D_MODEL), F32),
        compiler_params=_params(1), name="post",
    )(xf, ya, yb, yc, gates, pf, wa, wb, wc, wo, nffn, wg, wu, wd, nple, wpg, wpi, nfin)


def _rope_tables(seq):
    inv = 1.0 / (ROPE_THETA ** (jnp.arange(0, DH_A, 2, dtype=F32) / DH_A))
    ang = jnp.arange(seq, dtype=F32)[:, None] * inv[None, :]
    cos = jnp.tile(jnp.cos(ang), (1, 2 * LANES // DH_A))
    sin = jnp.sin(ang)
    sin = jnp.tile(jnp.concatenate([-sin, sin], axis=-1), (1, LANES // DH_A))
    return cos, sin


def kernel(x, p, norm_mix, w_in, lam_q1, lam_k1, lam_q2, lam_k2, subln_g, conv_w, conv_b, cln_g, cln_b, rpb, w_br_a, w_br_b, w_br_c, w_o, norm_ffn, w_ffn_gate, w_ffn_up, w_ffn_down, norm_ple, w_ple_in, w_ple_gate, norm_final):
    batch, seq, d = x.shape
    t = batch * seq
    assert d == D_MODEL and seq % GRID_W == 0
    rows = seq // GRID_W
    cos, sin = _rope_tables(seq)
    xf = x.reshape(t, d)
    vec = lambda a: a.reshape(1, -1)
    bf = lambda a: a.astype(BF16)
    for i in range(DEPTH):
        lam_init = 0.8 - 0.6 * math.exp(-0.3 * i)
        qa, kat, va, z, qc, kct, vc, gates = _proj(xf, vec(norm_mix[i]), bf(w_in[i]), cos, sin, batch, seq)
        ya = _diffattn(qa, kat, va, vec(lam_q1[i]), vec(lam_k1[i]), vec(lam_q2[i]), vec(lam_k2[i]),
                       vec(subln_g[i]), lam_init, batch, seq)
        yb = _conformer(z, conv_w[i].reshape(CONV_W, CB), vec(conv_b[i]), vec(cln_g[i]), vec(cln_b[i]),
                        batch, seq)
        yc = _natten(qc, kct, vc, _natten_bias(rpb[i], rows), batch, seq)
        xf = _post(xf, ya, yb, yc, gates, p[i].reshape(t, PLE_DIM),
                   bf(w_br_a[i]), bf(w_br_b[i]), bf(w_br_c[i]), bf(w_o[i]), vec(norm_ffn[i]),
                   bf(w_ffn_gate[i]), bf(w_ffn_up[i]), bf(w_ffn_down[i]), vec(norm_ple[i]),
                   bf(w_ple_gate[i]), bf(w_ple_in[i]), vec(norm_final), final=(i == DEPTH - 1))
    return xf.reshape(batch, seq, d)
```

```python
import functools
import math

import jax
import jax.numpy as jnp
from jax import lax
from jax.experimental import pallas as pl
from jax.experimental.pallas import tpu as pltpu

F32 = jnp.float32
BF16 = jnp.bfloat16

D_MODEL = 1024
DEPTH = 2
PLE_DIM = 256
GRID_W = 64
EPS = 1e-6
ROPE_THETA = 10000.0
HA = 4
DH_A = 64
DV_A = 2 * DH_A
WIDTH_A = HA * DV_A
CB = 512
CONV_W = 31
HC = 8
DC = 64
WIDTH_C = HC * DC
NA_KR = 8
NA_KC = 16
D_FF = 2816
D_IN = 7168
OFF_QA, OFF_KA, OFF_VA, OFF_GLU, OFF_QC, OFF_KC, OFF_VC, OFF_GATES = 0, 512, 1024, 1536, 2560, 3072, 3584, 4096

LANES = 128
VMEM_LIMIT_BYTES = 56 * 1024 * 1024

LOG2E = math.log2(math.e)
NEG_BIG = -1e30

TM_PROJ = 512
TQ_A = 512
TK_A = 1024
TS_CONV = 256
RC_CONV = 32
CONV_HALO = 16
NA_QROWS = 4
NA_KROWS = 12
TM_POST = 256


def _dot(a, b):
    return jnp.dot(a, b, preferred_element_type=F32)


def _sigmoid(t):
    return 1.0 / (1.0 + jnp.exp(-t))


def _rms(t, g):
    return t * lax.rsqrt(jnp.mean(t * t, axis=-1, keepdims=True) + EPS) * g


def _const_spec(shape):
    nd = len(shape)
    return pl.BlockSpec(shape, lambda *_: (0,) * nd, pipeline_mode=pl.Buffered(1))


def _params(n_axes):
    return pltpu.CompilerParams(
        dimension_semantics=("arbitrary",) * n_axes, vmem_limit_bytes=VMEM_LIMIT_BYTES)


def _proj_kernel(x_ref, g_ref, w_ref, cos_ref, sin_ref,
                 qa_ref, kat_ref, va_ref, z_ref, qc_ref, kct_ref, vc_ref, gates_ref):
    h = _rms(x_ref[...], g_ref[...]).astype(BF16)

    def mm(c0, n):
        return _dot(h, w_ref[:, c0:c0 + n])

    cos = cos_ref[...]
    sin = sin_ref[...]
    lane = lax.broadcasted_iota(jnp.int32, cos.shape, 1)
    low_half = (lane & (DH_A // 2)) == 0

    def rope(t):
        rot = jnp.where(low_half, pltpu.roll(t, LANES - DH_A // 2, 1), pltpu.roll(t, DH_A // 2, 1))
        return t * cos + rot * sin

    q = mm(OFF_QA, 512)
    k = mm(OFF_KA, 512)
    for c in range(HA):
        sl = slice(c * LANES, (c + 1) * LANES)
        qa_ref[:, sl] = (rope(q[:, sl]) * (DH_A ** -0.5 * LOG2E)).astype(BF16)
        kat_ref[0, sl, :] = rope(k[:, sl]).T.astype(BF16)
    va_ref[...] = mm(OFF_VA, 512).astype(BF16)
    z_ref[...] = mm(OFF_GLU, CB) * _sigmoid(mm(OFF_GLU + CB, CB))
    qc_ref[...] = (mm(OFF_QC, 512) * (DC ** -0.5 * LOG2E)).astype(BF16)
    kc = mm(OFF_KC, 512)
    for c in range(WIDTH_C // LANES):
        sl = slice(c * LANES, (c + 1) * LANES)
        kct_ref[0, sl, :] = kc[:, sl].T.astype(BF16)
    vc_ref[...] = mm(OFF_VC, 512).astype(BF16)
    for c in range(3 * D_MODEL // 512):
        gates_ref[:, c * 512:(c + 1) * 512] = _sigmoid(mm(OFF_GATES + c * 512, 512)).astype(BF16)


def _proj(xf, g, w, cos, sin, batch, seq):
    t = xf.shape[0]
    tm = TM_PROJ
    ns = seq // tm
    row = lambda i: (i, 0)
    tposed = lambda i: (i // ns, 0, i % ns)
    out_shape = (
        jax.ShapeDtypeStruct((t, 512), BF16),
        jax.ShapeDtypeStruct((batch, 512, seq), BF16),
        jax.ShapeDtypeStruct((t, WIDTH_A), BF16),
        jax.ShapeDtypeStruct((t, CB), F32),
        jax.ShapeDtypeStruct((t, WIDTH_C), BF16),
        jax.ShapeDtypeStruct((batch, WIDTH_C, seq), BF16),
        jax.ShapeDtypeStruct((t, WIDTH_C), BF16),
        jax.ShapeDtypeStruct((t, 3 * D_MODEL), BF16),
    )
    out_specs = (
        pl.BlockSpec((tm, 512), row),
        pl.BlockSpec((1, 512, tm), tposed),
        pl.BlockSpec((tm, WIDTH_A), row),
        pl.BlockSpec((tm, CB), row),
        pl.BlockSpec((tm, WIDTH_C), row),
        pl.BlockSpec((1, WIDTH_C, tm), tposed),
        pl.BlockSpec((tm, WIDTH_C), row),
        pl.BlockSpec((tm, 3 * D_MODEL), row),
    )
    in_specs = [
        pl.BlockSpec((tm, D_MODEL), row),
        _const_spec((1, D_MODEL)),
        _const_spec((D_MODEL, D_IN)),
        pl.BlockSpec((tm, LANES), lambda i: (i % ns, 0)),
        pl.BlockSpec((tm, LANES), lambda i: (i % ns, 0)),
    ]
    return pl.pallas_call(
        _proj_kernel, grid=(t // tm,), in_specs=in_specs, out_specs=out_specs,
        out_shape=out_shape, compiler_params=_params(1), name="proj",
    )(xf, g, w, cos, sin)


def _diffattn_kernel(q_ref, kt_ref, v_ref, lq1_ref, lk1_ref, lq2_ref, lk2_ref, g_ref,
                     o_ref, vext_ref, m_ref, acc_ref, *, lam_init, seq):
    tq = q_ref.shape[0]

    @pl.when(pl.program_id(2) == 0)
    def _():
        vext_ref[:, :DV_A] = v_ref[...]
        vext_ref[:, DV_A:] = jnp.ones((seq, DV_A), BF16)

    q = q_ref[...].astype(F32)
    lane = lax.broadcasted_iota(jnp.int32, q.shape, 1)
    qs = jnp.concatenate(
        [jnp.where(lane < DH_A, q, 0.0), jnp.where(lane >= DH_A, q, 0.0)], axis=0).astype(BF16)

    m_ref[...] = jnp.full(m_ref.shape, NEG_BIG, F32)
    acc_ref[...] = jnp.zeros(acc_ref.shape, F32)

    def chunk(kc, carry):
        off = pl.multiple_of(kc * TK_A, TK_A)
        s = _dot(qs, kt_ref[0, :, pl.ds(off, TK_A)])
        m_prev = m_ref[...]
        m_new = jnp.maximum(m_prev, jnp.max(s, axis=1, keepdims=True))
        alpha = jnp.exp2(m_prev - m_new)
        p = jnp.exp2(s - jnp.concatenate([m_new] * (TK_A // LANES), axis=1))
        pv = _dot(p.astype(BF16), vext_ref[pl.ds(off, TK_A), :])
        acc_ref[...] = acc_ref[...] * jnp.concatenate([alpha, alpha], axis=1) + pv
        m_ref[...] = m_new
        return carry

    lax.fori_loop(0, seq // TK_A, chunk, 0)

    acc = acc_ref[...]
    o1 = acc[:tq, :DV_A] / acc[:tq, DV_A:]
    o2 = acc[tq:, :DV_A] / acc[tq:, DV_A:]
    lam = (jnp.exp(jnp.sum(lq1_ref[...] * lk1_ref[...], axis=-1, keepdims=True))
           - jnp.exp(jnp.sum(lq2_ref[...] * lk2_ref[...], axis=-1, keepdims=True)) + lam_init)
    o = o1 - lam * o2
    o_ref[...] = (_rms(o, g_ref[...]) * (1.0 - lam_init)).astype(BF16)


def _diffattn(qa, kat, va, lq1, lk1, lq2, lk2, g, lam_init, batch, seq):
    t = qa.shape[0]
    nq = seq // TQ_A
    kern = functools.partial(_diffattn_kernel, lam_init=lam_init, seq=seq)
    small = lambda n: pl.BlockSpec((1, n), lambda b, h, i: (0, 0))
    return pl.pallas_call(
        kern, grid=(batch, HA, nq),
        in_specs=[
            pl.BlockSpec((TQ_A, DV_A), lambda b, h, i: (b * nq + i, h)),
            pl.BlockSpec((1, DV_A, seq), lambda b, h, i: (b, h, 0)),
            pl.BlockSpec((seq, DV_A), lambda b, h, i: (b, h)),
            small(DH_A), small(DH_A), small(DH_A), small(DH_A), small(DV_A),
        ],
        out_specs=pl.BlockSpec((TQ_A, DV_A), lambda b, h, i: (b * nq + i, h)),
        out_shape=jax.ShapeDtypeStruct((t, WIDTH_A), BF16),
        scratch_shapes=[
            pltpu.VMEM((seq, 2 * DV_A), BF16),
            pltpu.VMEM((2 * TQ_A, LANES), F32),
            pltpu.VMEM((2 * TQ_A, 2 * DV_A), F32),
        ],
        compiler_params=_params(3), name="diffattn",
    )(qa, kat, va, lq1, lk1, lq2, lk2, g)


def _conv_kernel(zp_ref, z_ref, zn_ref, w_ref, b_ref, g_ref, beta_ref, o_ref, buf_ref):
    i = pl.program_id(1)
    ts = z_ref.shape[1]
    buf_ref[0:CONV_HALO, :] = jnp.where(i > 0, zp_ref[0], 0.0)
    buf_ref[CONV_HALO:CONV_HALO + ts, :] = z_ref[0]
    buf_ref[CONV_HALO + ts:, :] = jnp.where(i < pl.num_programs(1) - 1, zn_ref[0], 0.0)
    base = CONV_HALO - CONV_W // 2
    for r in range(ts // RC_CONV):
        acc = jnp.zeros((RC_CONV, CB), F32)
        for k in range(CONV_W):
            r0 = base + r * RC_CONV + k
            acc = acc + buf_ref[r0:r0 + RC_CONV, :] * w_ref[k:k + 1, :]
        y = acc + b_ref[...]
        mu = jnp.mean(y, axis=-1, keepdims=True)
        d = y - mu
        yn = d * lax.rsqrt(jnp.mean(d * d, axis=-1, keepdims=True) + EPS) * g_ref[...] + beta_ref[...]
        o_ref[r * RC_CONV:(r + 1) * RC_CONV, :] = (yn * _sigmoid(yn)).astype(BF16)


def _conformer(z, w, b, g, beta, batch, seq):
    ts = TS_CONV
    ns = seq // ts
    hb = ts // CONV_HALO
    nh = seq // CONV_HALO
    z3 = z.reshape(batch, seq, CB)
    vec = lambda: pl.BlockSpec((1, CB), lambda bb, i: (0, 0))
    return pl.pallas_call(
        _conv_kernel, grid=(batch, ns),
        in_specs=[
            pl.BlockSpec((1, CONV_HALO, CB), lambda bb, i: (bb, jnp.maximum(i * hb - 1, 0), 0)),
            pl.BlockSpec((1, ts, CB), lambda bb, i: (bb, i, 0)),
            pl.BlockSpec((1, CONV_HALO, CB), lambda bb, i: (bb, jnp.minimum((i + 1) * hb, nh - 1), 0)),
            pl.BlockSpec((CONV_W, CB), lambda bb, i: (0, 0)),
            vec(), vec(), vec(),
        ],
        out_specs=pl.BlockSpec((ts, CB), lambda bb, i: (bb * ns + i, 0)),
        out_shape=jax.ShapeDtypeStruct((batch * seq, CB), BF16),
        scratch_shapes=[pltpu.VMEM((ts + 2 * CONV_HALO, CB), F32)],
        compiler_params=_params(2), name="conformer",
    )(z3, z3, z3, w, b, g, beta)


def _build_natten_bias(rpb_ref, bias_ref, rows):
    nq = NA_QROWS * GRID_W
    c = lax.broadcasted_iota(jnp.int32, (GRID_W, LANES), 0)
    lane = lax.broadcasted_iota(jnp.int32, (GRID_W, LANES), 1)
    kc = lane & (GRID_W - 1)
    cs = jnp.clip(c - NA_KC // 2, 0, GRID_W - NA_KC)
    col_ok = (kc >= cs) & (kc < cs + NA_KC)
    low = lane < GRID_W
    neg = jnp.full((GRID_W, LANES), NEG_BIG, F32)
    for hh in range(2):
        tiles = []
        for ro in range(2 * NA_KR - 1):
            xrow = jnp.broadcast_to(rpb_ref[hh, ro:ro + 1, :] * LOG2E, (GRID_W, LANES))
            lo = pltpu.roll(xrow, 0, 1, stride=1, stride_axis=0)
            hi = pltpu.roll(xrow, GRID_W, 1, stride=1, stride_axis=0)
            tiles.append(jnp.where(col_ok, jnp.where(low, lo, hi), neg))
        for var, q0 in enumerate((0, NA_QROWS, rows - NA_QROWS)):
            ws = min(max(q0 - NA_KR // 2, 0), rows - NA_KROWS)
            for rl in range(NA_QROWS):
                r = q0 + rl
                rstart = min(max(r - NA_KR // 2, 0), rows - NA_KR)
                for kp in range(NA_KROWS // 2):
                    halves = []
                    for kr in (ws + 2 * kp, ws + 2 * kp + 1):
                        inside = rstart <= kr < rstart + NA_KR
                        halves.append(tiles[kr - r + NA_KR - 1] if inside else neg)
                    r0 = hh * nq + rl * GRID_W
                    bias_ref[var, r0:r0 + GRID_W, kp * LANES:(kp + 1) * LANES] = jnp.where(low, halves[0], halves[1])


def _natten_kernel(q_ref, kt_ref, v_ref, rpb_ref, o_ref, bias_ref, *, rows):
    b = pl.program_id(1)
    rb = pl.program_id(2)
    nrb = pl.num_programs(2)
    nq = NA_QROWS * GRID_W
    nk = NA_KROWS * GRID_W

    @pl.when((b == 0) & (rb == 0))
    def _():
        _build_natten_bias(rpb_ref, bias_ref, rows)

    wstart = jnp.clip(rb * NA_QROWS - NA_KR // 2, 0, rows - NA_KROWS)
    off = pl.multiple_of(wstart * GRID_W, NA_QROWS * GRID_W)
    variant = jnp.where(rb == 0, 0, jnp.where(rb == nrb - 1, 2, 1))

    q = q_ref[...].astype(F32)
    lane = lax.broadcasted_iota(jnp.int32, q.shape, 1)
    qs = jnp.concatenate(
        [jnp.where(lane < DC, q, 0.0), jnp.where(lane >= DC, q, 0.0)], axis=0).astype(BF16)
    s = _dot(qs, kt_ref[0, :, pl.ds(off, nk)]) + bias_ref[variant]
    m = jnp.max(s, axis=-1, keepdims=True)
    p = jnp.exp2(s - m)
    l = jnp.sum(p, axis=-1, keepdims=True)
    o = _dot(p.astype(BF16), v_ref[pl.ds(off, nk), :]) / l
    o_ref[...] = jnp.where(lane < DC, o[:nq], o[nq:]).astype(BF16)


def _natten(qc, kct, vc, rpb_pad, batch, seq):
    t = qc.shape[0]
    rows = seq // GRID_W
    nrb = rows // NA_QROWS
    nq = NA_QROWS * GRID_W
    nk = NA_KROWS * GRID_W
    kern = functools.partial(_natten_kernel, rows=rows)
    return pl.pallas_call(
        kern, grid=(HC // 2, batch, nrb),
        in_specs=[
            pl.BlockSpec((nq, LANES), lambda hp, b, rb: (b * nrb + rb, hp)),
            pl.BlockSpec((1, LANES, seq), lambda hp, b, rb: (b, hp, 0)),
            pl.BlockSpec((seq, LANES), lambda hp, b, rb: (b, hp)),
            pl.BlockSpec((2,) + rpb_pad.shape[1:], lambda hp, b, rb: (hp, 0, 0)),
        ],
        out_specs=pl.BlockSpec((nq, LANES), lambda hp, b, rb: (b * nrb + rb, hp)),
        out_shape=jax.ShapeDtypeStruct((t, WIDTH_C), BF16),
        scratch_shapes=[pltpu.VMEM((3, 2 * nq, nk), F32)],
        compiler_params=_params(3), name="natten",
    )(qc, kct, vc, rpb_pad)


def _post_kernel(x_ref, ya_ref, yb_ref, yc_ref, gates_ref, p_ref,
                 wa_ref, wb_ref, wc_ref, wo_ref, nffn_ref, wg_ref, wu_ref, wd_ref,
                 nple_ref, wpg_ref, wpi_ref, nfin_ref, o_ref, *, final):
    x = x_ref[...]
    d = D_MODEL
    merged = (gates_ref[:, 0:d].astype(F32) * _dot(ya_ref[...], wa_ref[...])
              + gates_ref[:, d:2 * d].astype(F32) * _dot(yb_ref[...], wb_ref[...])
              + gates_ref[:, 2 * d:3 * d].astype(F32) * _dot(yc_ref[...], wc_ref[...]))
    x = x + _dot(merged.astype(BF16), wo_ref[...])

    h = _rms(x, nffn_ref[...]).astype(BF16)
    gate = _dot(h, wg_ref[...])
    up = _dot(h, wu_ref[...])
    act = (gate * _sigmoid(gate) * up).astype(BF16)
    x = x + _dot(act, wd_ref[...])

    h = _rms(x, nple_ref[...]).astype(BF16)
    sg = _sigmoid(_dot(h, wpg_ref[...]))
    x = x + sg * _dot(p_ref[...].astype(BF16), wpi_ref[...])
    if final:
        x = _rms(x, nfin_ref[...])
    o_ref[...] = x


def _post(xf, ya, yb, yc, gates, pf, wa, wb, wc, wo, nffn, wg, wu, wd, nple, wpg, wpi, nfin, final):
    t = xf.shape[0]
    tm = TM_POST
    row = lambda n: pl.BlockSpec((tm, n), lambda i: (i, 0))
    kern = functools.partial(_post_kernel, final=final)
    return pl.pallas_call(
        kern, grid=(t // tm,),
        in_specs=[
            row(D_MODEL), row(WIDTH_A), row(CB), row(WIDTH_C), row(3 * D_MODEL), row(PLE_DIM),
            _const_spec((WIDTH_A, D_MODEL)), _const_spec((CB, D_MODEL)), _const_spec((WIDTH_C, D_MODEL)),
            _const_spec((D_MODEL, D_MODEL)), _const_spec((1, D_MODEL)),
            _const_spec((D_MODEL, D_FF)), _const_spec((D_MODEL, D_FF)), _const_spec((D_FF, D_MODEL)),
            _const_spec((1, D_MODEL)), _const_spec((D_MODEL, D_MODEL)), _const_spec((PLE_DIM, D_MODEL)),
            _const_spec((1, D_MODEL)),
        ],
        out_specs=row(D_MODEL),
        out_shape=jax.ShapeDtypeStruct((t, D_MODEL), F32),
        compiler_params=_params(1), name="post",
    )(xf, ya, yb, yc, gates, pf, wa, wb, wc, wo, nffn, wg, wu, wd, nple, wpg, wpi, nfin)


def _rope_tables(seq):
    inv = 1.0 / (ROPE_THETA ** (jnp.arange(0, DH_A, 2, dtype=F32) / DH_A))
    ang = jnp.arange(seq, dtype=F32)[:, None] * inv[None, :]
    cos = jnp.tile(jnp.cos(ang), (1, 2 * LANES // DH_A))
    sin = jnp.sin(ang)
    sin = jnp.tile(jnp.concatenate([-sin, sin], axis=-1), (1, LANES // DH_A))
    return cos, sin


def kernel(x, p, norm_mix, w_in, lam_q1, lam_k1, lam_q2, lam_k2, subln_g, conv_w, conv_b, cln_g, cln_b, rpb, w_br_a, w_br_b, w_br_c, w_o, norm_ffn, w_ffn_gate, w_ffn_up, w_ffn_down, norm_ple, w_ple_in, w_ple_gate, norm_final):
    batch, seq, d = x.shape
    t = batch * seq
    assert d == D_MODEL and seq % GRID_W == 0
    rows = seq // GRID_W
    cos, sin = _rope_tables(seq)
    xf = x.reshape(t, d)
    vec = lambda a: a.reshape(1, -1)
    bf = lambda a: a.astype(BF16)
    for i in range(DEPTH):
        lam_init = 0.8 - 0.6 * math.exp(-0.3 * i)
        qa, kat, va, z, qc, kct, vc, gates = _proj(xf, vec(norm_mix[i]), bf(w_in[i]), cos, sin, batch, seq)
        ya = _diffattn(qa, kat, va, vec(lam_q1[i]), vec(lam_k1[i]), vec(lam_q2[i]), vec(lam_k2[i]),
                       vec(subln_g[i]), lam_init, batch, seq)
        yb = _conformer(z, conv_w[i].reshape(CONV_W, CB), vec(conv_b[i]), vec(cln_g[i]), vec(cln_b[i]),
                        batch, seq)
        rpb_pad = jnp.pad(rpb[i], ((0, 0), (0, 1), (0, LANES - (2 * NA_KC - 1))))
        rpb_pad = jnp.roll(rpb_pad, -(NA_KC - 1), axis=-1)
        yc = _natten(qc, kct, vc, rpb_pad, batch, seq)
        xf = _post(xf, ya, yb, yc, gates, p[i].reshape(t, PLE_DIM),
                   bf(w_br_a[i]), bf(w_br_b[i]), bf(w_br_c[i]), bf(w_o[i]), vec(norm_ffn[i]),
                   bf(w_ffn_gate[i]), bf(w_ffn_up[i]), bf(w_ffn_down[i]), vec(norm_ple[i]),
                   bf(w_ple_gate[i]), bf(w_ple_in[i]), vec(norm_final), final=(i == DEPTH - 1))
    return xf.reshape(batch, seq, d)
```

```python
import functools
import math

import jax
import jax.numpy as jnp
from jax import lax
from jax.experimental import pallas as pl
from jax.experimental.pallas import tpu as pltpu

F32 = jnp.float32
BF16 = jnp.bfloat16

D_MODEL = 1024
DEPTH = 2
PLE_DIM = 256
GRID_W = 64
EPS = 1e-6
ROPE_THETA = 10000.0
HA = 4
DH_A = 64
DV_A = 2 * DH_A
WIDTH_A = HA * DV_A
CB = 512
CONV_W = 31
HC = 8
DC = 64
WIDTH_C = HC * DC
NA_KR = 8
NA_KC = 16
D_FF = 2816
D_IN = 7168
OFF_QA, OFF_KA, OFF_VA, OFF_GLU, OFF_QC, OFF_KC, OFF_VC, OFF_GATES = 0, 512, 1024, 1536, 2560, 3072, 3584, 4096

LANES = 128
VMEM_LIMIT_BYTES = 56 * 1024 * 1024

LOG2E = math.log2(math.e)
NEG_BIG = -1e30

TM_PROJ = 512
TQ_A = 1024
QB_A = 128
TK_A = 1024
TS_CONV = 256
RC_CONV = 64
LC_CONV = 256
CONV_HALO = 16
NA_QROWS = 4
NA_SUB = 4
NA_KROWS = 12
TM_POST = 256


def _dot(a, b):
    return jnp.dot(a, b, preferred_element_type=F32)


def _sigmoid(t):
    return 1.0 / (1.0 + jnp.exp(-t))


def _rms(t, g):
    return t * lax.rsqrt(jnp.mean(t * t, axis=-1, keepdims=True) + EPS) * g


def _const_spec(shape):
    nd = len(shape)
    return pl.BlockSpec(shape, lambda *_: (0,) * nd, pipeline_mode=pl.Buffered(1))


def _params(n_axes):
    return pltpu.CompilerParams(
        dimension_semantics=("arbitrary",) * n_axes, vmem_limit_bytes=VMEM_LIMIT_BYTES)


def _proj_kernel(x_ref, g_ref, w_ref, cos_ref, sin_ref,
                 qa_ref, kat_ref, va_ref, z_ref, qc_ref, kct_ref, vc_ref, gates_ref):
    h = _rms(x_ref[...], g_ref[...]).astype(BF16)

    def mm(c0, n):
        return _dot(h, w_ref[:, c0:c0 + n])

    cos = cos_ref[...]
    sin = sin_ref[...]
    lane = lax.broadcasted_iota(jnp.int32, cos.shape, 1)
    low_half = (lane & (DH_A // 2)) == 0

    def rope(t):
        rot = jnp.where(low_half, pltpu.roll(t, LANES - DH_A // 2, 1), pltpu.roll(t, DH_A // 2, 1))
        return t * cos + rot * sin

    q = mm(OFF_QA, 512)
    k = mm(OFF_KA, 512)
    for c in range(HA):
        sl = slice(c * LANES, (c + 1) * LANES)
        qa_ref[:, sl] = (rope(q[:, sl]) * (DH_A ** -0.5 * LOG2E)).astype(BF16)
        kat_ref[0, sl, :] = rope(k[:, sl]).T.astype(BF16)
    va_ref[...] = mm(OFF_VA, 512).astype(BF16)
    z_ref[...] = mm(OFF_GLU, CB) * _sigmoid(mm(OFF_GLU + CB, CB))
    qc_ref[...] = (mm(OFF_QC, 512) * (DC ** -0.5 * LOG2E)).astype(BF16)
    kc = mm(OFF_KC, 512)
    for c in range(WIDTH_C // LANES):
        sl = slice(c * LANES, (c + 1) * LANES)
        kct_ref[0, sl, :] = kc[:, sl].T.astype(BF16)
    vc_ref[...] = mm(OFF_VC, 512).astype(BF16)
    for c in range(3 * D_MODEL // 512):
        gates_ref[:, c * 512:(c + 1) * 512] = _sigmoid(mm(OFF_GATES + c * 512, 512)).astype(BF16)


def _proj(xf, g, w, cos, sin, batch, seq):
    t = xf.shape[0]
    tm = TM_PROJ
    ns = seq // tm
    row = lambda i: (i, 0)
    tposed = lambda i: (i // ns, 0, i % ns)
    out_shape = (
        jax.ShapeDtypeStruct((t, 512), BF16),
        jax.ShapeDtypeStruct((batch, 512, seq), BF16),
        jax.ShapeDtypeStruct((t, WIDTH_A), BF16),
        jax.ShapeDtypeStruct((t, CB), F32),
        jax.ShapeDtypeStruct((t, WIDTH_C), BF16),
        jax.ShapeDtypeStruct((batch, WIDTH_C, seq), BF16),
        jax.ShapeDtypeStruct((t, WIDTH_C), BF16),
        jax.ShapeDtypeStruct((t, 3 * D_MODEL), BF16),
    )
    out_specs = (
        pl.BlockSpec((tm, 512), row),
        pl.BlockSpec((1, 512, tm), tposed),
        pl.BlockSpec((tm, WIDTH_A), row),
        pl.BlockSpec((tm, CB), row),
        pl.BlockSpec((tm, WIDTH_C), row),
        pl.BlockSpec((1, WIDTH_C, tm), tposed),
        pl.BlockSpec((tm, WIDTH_C), row),
        pl.BlockSpec((tm, 3 * D_MODEL), row),
    )
    in_specs = [
        pl.BlockSpec((tm, D_MODEL), row),
        _const_spec((1, D_MODEL)),
        _const_spec((D_MODEL, D_IN)),
        pl.BlockSpec((tm, LANES), lambda i: (i % ns, 0)),
        pl.BlockSpec((tm, LANES), lambda i: (i % ns, 0)),
    ]
    return pl.pallas_call(
        _proj_kernel, grid=(t // tm,), in_specs=in_specs, out_specs=out_specs,
        out_shape=out_shape, compiler_params=_params(1), name="proj",
    )(xf, g, w, cos, sin)


def _diffattn_kernel(q_ref, kt_ref, v_ref, lq1_ref, lk1_ref, lq2_ref, lk2_ref, g_ref,
                     o_ref, vext_ref, *, lam_init, seq):
    @pl.when(pl.program_id(2) == 0)
    def _():
        vext_ref[:, :DV_A] = v_ref[...]
        vext_ref[:, DV_A:] = jnp.ones((seq, DV_A), BF16)

    lam = (jnp.exp(jnp.sum(lq1_ref[...] * lk1_ref[...], axis=-1, keepdims=True))
           - jnp.exp(jnp.sum(lq2_ref[...] * lk2_ref[...], axis=-1, keepdims=True)) + lam_init)
    lane = lax.broadcasted_iota(jnp.int32, (QB_A, DV_A), 1)

    for qb in range(q_ref.shape[0] // QB_A):
        rows = slice(qb * QB_A, (qb + 1) * QB_A)
        q = q_ref[rows, :].astype(F32)
        qs = jnp.concatenate(
            [jnp.where(lane < DH_A, q, 0.0), jnp.where(lane >= DH_A, q, 0.0)], axis=0).astype(BF16)
        m = acc = None
        for kc in range(seq // TK_A):
            keys = slice(kc * TK_A, (kc + 1) * TK_A)
            s = _dot(qs, kt_ref[0, :, keys])
            mx = jnp.max(s, axis=1, keepdims=True)
            if kc == 0:
                m = mx
                acc = _dot(jnp.exp2(s - m).astype(BF16), vext_ref[keys, :])
            else:
                m_new = jnp.maximum(m, mx)
                acc = (acc * jnp.exp2(m - m_new)
                       + _dot(jnp.exp2(s - m_new).astype(BF16), vext_ref[keys, :]))
                m = m_new
        o1 = acc[:QB_A, :DV_A] / acc[:QB_A, DV_A:]
        o2 = acc[QB_A:, :DV_A] / acc[QB_A:, DV_A:]
        o = o1 - lam * o2
        o_ref[rows, :] = (_rms(o, g_ref[...]) * (1.0 - lam_init)).astype(BF16)


def _diffattn(qa, kat, va, lq1, lk1, lq2, lk2, g, lam_init, batch, seq):
    t = qa.shape[0]
    nq = seq // TQ_A
    kern = functools.partial(_diffattn_kernel, lam_init=lam_init, seq=seq)
    small = lambda n: pl.BlockSpec((1, n), lambda b, h, i: (0, 0))
    return pl.pallas_call(
        kern, grid=(batch, HA, nq),
        in_specs=[
            pl.BlockSpec((TQ_A, DV_A), lambda b, h, i: (b * nq + i, h)),
            pl.BlockSpec((1, DV_A, seq), lambda b, h, i: (b, h, 0)),
            pl.BlockSpec((seq, DV_A), lambda b, h, i: (b, h)),
            small(DH_A), small(DH_A), small(DH_A), small(DH_A), small(DV_A),
        ],
        out_specs=pl.BlockSpec((TQ_A, DV_A), lambda b, h, i: (b * nq + i, h)),
        out_shape=jax.ShapeDtypeStruct((t, WIDTH_A), BF16),
        scratch_shapes=[pltpu.VMEM((seq, 2 * DV_A), BF16)],
        compiler_params=_params(3), name="diffattn",
    )(qa, kat, va, lq1, lk1, lq2, lk2, g)


def _conv_kernel(zp_ref, z_ref, zn_ref, w_ref, b_ref, g_ref, beta_ref, o_ref, buf_ref):
    i = pl.program_id(1)
    ts = z_ref.shape[1]
    buf_ref[0:CONV_HALO, :] = jnp.where(i > 0, zp_ref[0], 0.0)
    buf_ref[CONV_HALO:CONV_HALO + ts, :] = z_ref[0]
    buf_ref[CONV_HALO + ts:, :] = jnp.where(i < pl.num_programs(1) - 1, zn_ref[0], 0.0)
    base = CONV_HALO - CONV_W // 2
    sub = 8
    for r in range(ts // RC_CONV):
        halves = []
        for lh in range(CB // LC_CONV):
            lanes = slice(lh * LC_CONV, (lh + 1) * LC_CONV)
            acc = None
            for res in range(sub):
                part = None
                for o in range(res, base + CONV_W, sub):
                    if o < base:
                        continue
                    r0 = r * RC_CONV + o - res
                    term = buf_ref[r0:r0 + RC_CONV + sub, lanes] * w_ref[o - base:o - base + 1, lanes]
                    part = term if part is None else part + term
                part = part[res:res + RC_CONV, :]
                acc = part if acc is None else acc + part
            halves.append(acc)
        y = jnp.concatenate(halves, axis=1) + b_ref[...]
        mu = jnp.mean(y, axis=-1, keepdims=True)
        d = y - mu
        yn = d * lax.rsqrt(jnp.mean(d * d, axis=-1, keepdims=True) + EPS) * g_ref[...] + beta_ref[...]
        o_ref[r * RC_CONV:(r + 1) * RC_CONV, :] = (yn * _sigmoid(yn)).astype(BF16)


def _conformer(z, w, b, g, beta, batch, seq):
    ts = TS_CONV
    ns = seq // ts
    hb = ts // CONV_HALO
    nh = seq // CONV_HALO
    z3 = z.reshape(batch, seq, CB)
    vec = lambda: pl.BlockSpec((1, CB), lambda bb, i: (0, 0))
    return pl.pallas_call(
        _conv_kernel, grid=(batch, ns),
        in_specs=[
            pl.BlockSpec((1, CONV_HALO, CB), lambda bb, i: (bb, jnp.maximum(i * hb - 1, 0), 0)),
            pl.BlockSpec((1, ts, CB), lambda bb, i: (bb, i, 0)),
            pl.BlockSpec((1, CONV_HALO, CB), lambda bb, i: (bb, jnp.minimum((i + 1) * hb, nh - 1), 0)),
            pl.BlockSpec((CONV_W, CB), lambda bb, i: (0, 0)),
            vec(), vec(), vec(),
        ],
        out_specs=pl.BlockSpec((ts, CB), lambda bb, i: (bb * ns + i, 0)),
        out_shape=jax.ShapeDtypeStruct((batch * seq, CB), BF16),
        scratch_shapes=[pltpu.VMEM((ts + 2 * CONV_HALO, CB), F32)],
        compiler_params=_params(2), name="conformer",
    )(z3, z3, z3, w, b, g, beta)


def _build_natten_bias(rpb_ref, bias_ref, rows):
    nq = NA_QROWS * GRID_W
    c = lax.broadcasted_iota(jnp.int32, (GRID_W, LANES), 0)
    lane = lax.broadcasted_iota(jnp.int32, (GRID_W, LANES), 1)
    kc = lane & (GRID_W - 1)
    cs = jnp.clip(c - NA_KC // 2, 0, GRID_W - NA_KC)
    col_ok = (kc >= cs) & (kc < cs + NA_KC)
    low = lane < GRID_W
    neg = jnp.full((GRID_W, LANES), NEG_BIG, F32)
    for hh in range(2):
        tiles = []
        for ro in range(2 * NA_KR - 1):
            xrow = jnp.broadcast_to(rpb_ref[hh, ro:ro + 1, :] * LOG2E, (GRID_W, LANES))
            lo = pltpu.roll(xrow, 0, 1, stride=1, stride_axis=0)
            hi = pltpu.roll(xrow, GRID_W, 1, stride=1, stride_axis=0)
            tiles.append(jnp.where(col_ok, jnp.where(low, lo, hi), neg))
        for var, q0 in enumerate((0, NA_QROWS, rows - NA_QROWS)):
            ws = min(max(q0 - NA_KR // 2, 0), rows - NA_KROWS)
            for rl in range(NA_QROWS):
                r = q0 + rl
                rstart = min(max(r - NA_KR // 2, 0), rows - NA_KR)
                for kp in range(NA_KROWS // 2):
                    halves = []
                    for kr in (ws + 2 * kp, ws + 2 * kp + 1):
                        inside = rstart <= kr < rstart + NA_KR
                        halves.append(tiles[kr - r + NA_KR - 1] if inside else neg)
                    r0 = hh * nq + rl * GRID_W
                    bias_ref[var, r0:r0 + GRID_W, kp * LANES:(kp + 1) * LANES] = jnp.where(low, halves[0], halves[1])


def _natten_kernel(q_ref, kt_ref, v_ref, rpb_ref, o_ref, bias_ref, *, rows):
    b = pl.program_id(1)
    rb = pl.program_id(2)
    nrb = pl.num_programs(2)
    nq = NA_QROWS * GRID_W
    nk = NA_KROWS * GRID_W

    @pl.when((b == 0) & (rb == 0))
    def _():
        _build_natten_bias(rpb_ref, bias_ref, rows)

    lane = lax.broadcasted_iota(jnp.int32, (nq, LANES), 1)
    nblocks = nrb * NA_SUB
    for j in range(NA_SUB):
        blk = rb * NA_SUB + j
        wstart = jnp.clip(blk * NA_QROWS - NA_KR // 2, 0, rows - NA_KROWS)
        off = pl.multiple_of(wstart * GRID_W, NA_QROWS * GRID_W)
        variant = jnp.where(blk == 0, 0, jnp.where(blk == nblocks - 1, 2, 1))
        q = q_ref[j * nq:(j + 1) * nq, :].astype(F32)
        qs = jnp.concatenate(
            [jnp.where(lane < DC, q, 0.0), jnp.where(lane >= DC, q, 0.0)], axis=0).astype(BF16)
        s = _dot(qs, kt_ref[0, :, pl.ds(off, nk)]) + bias_ref[variant]
        m = jnp.max(s, axis=-1, keepdims=True)
        p = jnp.exp2(s - m)
        l = jnp.sum(p, axis=-1, keepdims=True)
        o = _dot(p.astype(BF16), v_ref[pl.ds(off, nk), :]) / l
        o_ref[j * nq:(j + 1) * nq, :] = jnp.where(lane < DC, o[:nq], o[nq:]).astype(BF16)


def _natten(qc, kct, vc, rpb_pad, batch, seq):
    t = qc.shape[0]
    rows = seq // GRID_W
    nrb = rows // (NA_QROWS * NA_SUB)
    nq = NA_SUB * NA_QROWS * GRID_W
    nk = NA_KROWS * GRID_W
    kern = functools.partial(_natten_kernel, rows=rows)
    return pl.pallas_call(
        kern, grid=(HC // 2, batch, nrb),
        in_specs=[
            pl.BlockSpec((nq, LANES), lambda hp, b, rb: (b * nrb + rb, hp)),
            pl.BlockSpec((1, LANES, seq), lambda hp, b, rb: (b, hp, 0)),
            pl.BlockSpec((seq, LANES), lambda hp, b, rb: (b, hp)),
            pl.BlockSpec((2,) + rpb_pad.shape[1:], lambda hp, b, rb: (hp, 0, 0)),
        ],
        out_specs=pl.BlockSpec((nq, LANES), lambda hp, b, rb: (b * nrb + rb, hp)),
        out_shape=jax.ShapeDtypeStruct((t, WIDTH_C), BF16),
        scratch_shapes=[pltpu.VMEM((3, 2 * NA_QROWS * GRID_W, nk), F32)],
        compiler_params=_params(3), name="natten",
    )(qc, kct, vc, rpb_pad)


def _post_kernel(x_ref, ya_ref, yb_ref, yc_ref, gates_ref, p_ref,
                 wa_ref, wb_ref, wc_ref, wo_ref, nffn_ref, wg_ref, wu_ref, wd_ref,
                 nple_ref, wpg_ref, wpi_ref, nfin_ref, o_ref, *, final):
    x = x_ref[...]
    d = D_MODEL
    merged = (gates_ref[:, 0:d].astype(F32) * _dot(ya_ref[...], wa_ref[...])
              + gates_ref[:, d:2 * d].astype(F32) * _dot(yb_ref[...], wb_ref[...])
              + gates_ref[:, 2 * d:3 * d].astype(F32) * _dot(yc_ref[...], wc_ref[...]))
    x = x + _dot(merged.astype(BF16), wo_ref[...])

    h = _rms(x, nffn_ref[...]).astype(BF16)
    gate = _dot(h, wg_ref[...])
    up = _dot(h, wu_ref[...])
    act = (gate * _sigmoid(gate) * up).astype(BF16)
    x = x + _dot(act, wd_ref[...])

    h = _rms(x, nple_ref[...]).astype(BF16)
    sg = _sigmoid(_dot(h, wpg_ref[...]))
    x = x + sg * _dot(p_ref[...].astype(BF16), wpi_ref[...])
    if final:
        x = _rms(x, nfin_ref[...])
    o_ref[...] = x


def _post(xf, ya, yb, yc, gates, pf, wa, wb, wc, wo, nffn, wg, wu, wd, nple, wpg, wpi, nfin, final):
    t = xf.shape[0]
    tm = TM_POST
    row = lambda n: pl.BlockSpec((tm, n), lambda i: (i, 0))
    kern = functools.partial(_post_kernel, final=final)
    return pl.pallas_call(
        kern, grid=(t // tm,),
        in_specs=[
            row(D_MODEL), row(WIDTH_A), row(CB), row(WIDTH_C), row(3 * D_MODEL), row(PLE_DIM),
            _const_spec((WIDTH_A, D_MODEL)), _const_spec((CB, D_MODEL)), _const_spec((WIDTH_C, D_MODEL)),
            _const_spec((D_MODEL, D_MODEL)), _const_spec((1, D_MODEL)),
            _const_spec((D_MODEL, D_FF)), _const_spec((D_MODEL, D_FF)), _const_spec((D_FF, D_MODEL)),
            _const_spec((1, D_MODEL)), _const_spec((D_MODEL, D_MODEL)), _const_spec((PLE_DIM, D_MODEL)),
            _const_spec((1, D_MODEL)),
        ],
        out_specs=row(D_MODEL),
        out_shape=jax.ShapeDtypeStruct((t, D_MODEL), F32),
        compiler_params=_params(1), name="post",
    )(xf, ya, yb, yc, gates, pf, wa, wb, wc, wo, nffn, wg, wu, wd, nple, wpg, wpi, nfin)


def _rope_tables(seq):
    inv = 1.0 / (ROPE_THETA ** (jnp.arange(0, DH_A, 2, dtype=F32) / DH_A))
    ang = jnp.arange(seq, dtype=F32)[:, None] * inv[None, :]
    cos = jnp.tile(jnp.cos(ang), (1, 2 * LANES // DH_A))
    sin = jnp.sin(ang)
    sin = jnp.tile(jnp.concatenate([-sin, sin], axis=-1), (1, LANES // DH_A))
    return cos, sin


def kernel(x, p, norm_mix, w_in, lam_q1, lam_k1, lam_q2, lam_k2, subln_g, conv_w, conv_b, cln_g, cln_b, rpb, w_br_a, w_br_b, w_br_c, w_o, norm_ffn, w_ffn_gate, w_ffn_up, w_ffn_down, norm_ple, w_ple_in, w_ple_gate, norm_final):
    batch, seq, d = x.shape
    t = batch * seq
    assert d == D_MODEL and seq % GRID_W == 0
    rows = seq // GRID_W
    cos, sin = _rope_tables(seq)
    xf = x.reshape(t, d)
    vec = lambda a: a.reshape(1, -1)
    bf = lambda a: a.astype(BF16)
    for i in range(DEPTH):
        lam_init = 0.8 - 0.6 * math.exp(-0.3 * i)
        qa, kat, va, z, qc, kct, vc, gates = _proj(xf, vec(norm_mix[i]), bf(w_in[i]), cos, sin, batch, seq)
        ya = _diffattn(qa, kat, va, vec(lam_q1[i]), vec(lam_k1[i]), vec(lam_q2[i]), vec(lam_k2[i]),
                       vec(subln_g[i]), lam_init, batch, seq)
        yb = _conformer(z, conv_w[i].reshape(CONV_W, CB), vec(conv_b[i]), vec(cln_g[i]), vec(cln_b[i]),
                        batch, seq)
        rpb_pad = jnp.pad(rpb[i], ((0, 0), (0, 1), (0, LANES - (2 * NA_KC - 1))))
        rpb_pad = jnp.roll(rpb_pad, -(NA_KC - 1), axis=-1)
        yc = _natten(qc, kct, vc, rpb_pad, batch, seq)
        xf = _post(xf, ya, yb, yc, gates, p[i].reshape(t, PLE_DIM),
                   bf(w_br_a[i]), bf(w_br_b[i]), bf(w_br_c[i]), bf(w_o[i]), vec(norm_ffn[i]),
                   bf(w_ffn_gate[i]), bf(w_ffn_up[i]), bf(w_ffn_down[i]), vec(norm_ple[i]),
                   bf(w_ple_gate[i]), bf(w_ple_in[i]), vec(norm_final), final=(i == DEPTH - 1))
    return xf.reshape(batch, seq, d)
```
